```python
import math
import jax
import jax.numpy as jnp
from jax import lax
import numpy as np

D_MODEL = 1024
BATCH = 2
SEQ = 16384
DEPTH = 2

GRID_W = 64
CTX_LEN = 256
N_MIXERS = 2
NORM_EPS = 1e-6
POS_BASE = 10000.0
GDN_HEADS = 8
GDN_DK = 128
GDN_DV = 128
GDN_CONV = 5
GDN_CHUNK = 64
GDN_QKV = GDN_HEADS * (2 * GDN_DK + GDN_DV)
GDN_Z = GDN_HEADS * GDN_DV
GDN_IN = GDN_QKV + GDN_Z + 4 * GDN_HEADS
FNET_GROUPS = 4
N_EXPERTS = 32
TOP_K = 4
D_FF = 1024
SWIGLU_LIMIT = 7.0
SWIGLU_ALPHA = 1.702
MOE_BLOCK = 256

kernel_name = 'hybrid_gdn_fnet_moe_dit'

F32 = jnp.float32


def _rmsnorm(x, gain):
    xf = x.astype(F32)
    y = xf * lax.rsqrt(jnp.mean(xf * xf, axis=-1, keepdims=True) + NORM_EPS)
    return (y * gain.astype(F32)).astype(x.dtype)


def _modulate(x, gain, shift, scale):
    return _rmsnorm(x, gain) * (1 + scale) + shift


def _adaln(cvec, w, b):
    return jnp.split(jax.nn.silu(cvec) @ w + b, 6, axis=-1)


def _sincos(pos, dim):
    half = dim // 2
    omega = 1.0 / (POS_BASE ** (jnp.arange(half, dtype=F32) / half))
    ang = pos.astype(F32)[:, None] * omega[None, :]
    return jnp.concatenate([jnp.sin(ang), jnp.cos(ang)], axis=-1)


def _grid_pos_embed(n_tok, dim):
    rows = n_tok // GRID_W
    r = jnp.repeat(jnp.arange(rows), GRID_W)
    col = jnp.tile(jnp.arange(GRID_W), rows)
    return jnp.concatenate([_sincos(r, dim // 2), _sincos(col, dim // 2)], axis=-1)


def _short_conv(u, w):
    n = u.shape[1]
    p = GDN_CONV // 2
    up = jnp.pad(u, ((0, 0), (p, p), (0, 0)))
    return sum(up[:, j:j + n] * w[j] for j in range(GDN_CONV))


def _l2norm(t):
    return t * lax.rsqrt(jnp.sum(t * t, axis=-1, keepdims=True) + NORM_EPS)


def _gdn_chunk_scan(q, k, v, g, beta, s0):
    b, h, n_tok, dk = q.shape
    dv = v.shape[-1]
    C = GDN_CHUNK
    n = n_tok // C
    q = q.reshape(b, h, n, C, dk)
    k = k.reshape(b, h, n, C, dk)
    v = v.reshape(b, h, n, C, dv)
    g = g.reshape(b, h, n, C)
    beta = beta.reshape(b, h, n, C)
    gc = jnp.cumsum(g, axis=-1)
    lower = jnp.tril(jnp.ones((C, C), bool))
    strict = jnp.tril(jnp.ones((C, C), bool), -1)
    decay = jnp.where(lower, jnp.exp(jnp.where(lower, gc[..., :, None] - gc[..., None, :], 0.0)), 0.0)
    kb = k * beta[..., None]
    a_mat = jnp.where(strict, jnp.einsum('bhncd,bhnsd->bhncs', kb, k) * decay, 0.0) + jnp.eye(C, dtype=F32)
    rhs = jnp.concatenate([v * beta[..., None], kb * jnp.exp(gc)[..., None]], axis=-1)
    sol = lax.linalg.triangular_solve(a_mat, rhs, left_side=True, lower=True, unit_diagonal=True)
    u, w = sol[..., :dv], sol[..., dv:]
    attn = jnp.where(lower, jnp.einsum('bhncd,bhnsd->bhncs', q, k) * decay, 0.0)

    def step(S, inp):
        q_i, k_i, u_i, w_i, gc_i, attn_i = inp
        v_new = u_i - jnp.einsum('bhck,bhkv->bhcv', w_i, S)
        o_i = (jnp.einsum('bhck,bhkv->bhcv', q_i * jnp.exp(gc_i)[..., None], S)
               + jnp.einsum('bhcs,bhsv->bhcv', attn_i, v_new))
        g_last = gc_i[..., -1]
        S = (S * jnp.exp(g_last)[..., None, None]
             + jnp.einsum('bhck,bhcv->bhkv', k_i * jnp.exp(g_last[..., None] - gc_i)[..., None], v_new))
        return S, o_i

    xs = tuple(jnp.moveaxis(t, 2, 0) for t in (q, k, u, w, gc, attn))
    S, o = lax.scan(step, s0, xs)
    return jnp.moveaxis(o, 0, 2).reshape(b, h, n_tok, dv), S


def _gdn_project(h, w_in, conv_w, a_log, dt_bias):
    b, n, _ = h.shape
    proj = h @ w_in
    qkv = jax.nn.silu(_short_conv(proj[..., :GDN_QKV], conv_w)).astype(F32)
    z = proj[..., GDN_QKV:GDN_QKV + GDN_Z]
    gates = proj[..., GDN_QKV + GDN_Z:].astype(F32).reshape(b, n, 4, GDN_HEADS)
    gates = jnp.transpose(gates, (2, 0, 3, 1))
    hk = GDN_HEADS * GDN_DK

    def heads(t, d):
        return jnp.transpose(t.reshape(b, n, GDN_HEADS, d), (0, 2, 1, 3))

    q = _l2norm(heads(qkv[..., :hk], GDN_DK)) * (GDN_DK ** -0.5)
    k = _l2norm(heads(qkv[..., hk:2 * hk], GDN_DK))
    v = heads(qkv[..., 2 * hk:], GDN_DV)
    beta = jax.nn.sigmoid(gates[:2])
    g = -jnp.exp(a_log.astype(F32))[:, None, :, None] * jax.nn.softplus(
        gates[2:] + dt_bias.astype(F32)[:, None, :, None])
    return q, k, v, z, beta, g


def _gdn_bidir(q, k, v, beta, g, s_fwd, s_bwd):
    o_f, s_f = _gdn_chunk_scan(q, k, v, g[0], beta[0], s_fwd)

    def fl(t):
        return jnp.flip(t, axis=2)

    o_b, s_b = _gdn_chunk_scan(fl(q), fl(k), fl(v), fl(g[1]), fl(beta[1]), s_bwd)
    return o_f + fl(o_b), s_f, s_b


def _gdn_output(o, z, onorm, w_out):
    b, hh, n, dv = o.shape
    o = jnp.transpose(o, (0, 2, 1, 3))
    o = o * lax.rsqrt(jnp.mean(o * o, axis=-1, keepdims=True) + NORM_EPS) * onorm.astype(F32)
    o = o.reshape(b, n, hh * dv) * jax.nn.silu(z.astype(F32))
    return o.astype(z.dtype) @ w_out


def _fourier_mix(h, w_out):
    b, n, d = h.shape
    hg = jnp.transpose(h.astype(F32).reshape(b, n, FNET_GROUPS, d // FNET_GROUPS), (0, 2, 1, 3))
    f = jnp.fft.fft2(hg, norm='ortho').real
    return jnp.transpose(f, (0, 2, 1, 3)).reshape(b, n, d).astype(h.dtype) @ w_out


def _moe(h, w_r, b_r, w_gu, b_gu, w_dn, b_dn):
    n_tok, d = h.shape
    logits = (h @ w_r + b_r).astype(F32)
    top_val, top_idx = lax.top_k(logits, TOP_K)
    top_w = jax.nn.softmax(top_val, axis=-1)
    n_as = n_tok * TOP_K
    flat_e = top_idx.reshape(-1)
    flat_tok = jnp.arange(n_as, dtype=jnp.int32) // TOP_K
    flat_w = top_w.reshape(-1)
    order = jnp.argsort(flat_e)
    e_sorted = flat_e[order]
    counts = jnp.bincount(flat_e, length=N_EXPERTS)
    padded = (counts + MOE_BLOCK - 1) // MOE_BLOCK * MOE_BLOCK
    pad_end = jnp.cumsum(padded)
    pad_start = pad_end - padded
    start = jnp.cumsum(counts) - counts
    dest = pad_start[e_sorted] + (jnp.arange(n_as) - start[e_sorted])
    n_blk = -(-(n_as + N_EXPERTS * (MOE_BLOCK - 1)) // MOE_BLOCK)
    n_pad = n_blk * MOE_BLOCK
    tok_buf = jnp.full((n_pad,), n_tok, jnp.int32).at[dest].set(flat_tok[order])
    w_buf = jnp.zeros((n_pad,), F32).at[dest].set(flat_w[order])
    blk_e = jnp.minimum(jnp.searchsorted(pad_end, jnp.arange(n_blk) * MOE_BLOCK, side='right'), N_EXPERTS - 1)
    h_pad = jnp.concatenate([h, jnp.zeros((1, d), h.dtype)], axis=0)
    xs = h_pad[tok_buf].reshape(n_blk, MOE_BLOCK, d)

    def expert_block(args):
        xb, e = args
        gu = xb @ w_gu[e] + b_gu[e]
        gate = jnp.minimum(gu[:, ::2], SWIGLU_LIMIT)
        up = jnp.clip(gu[:, 1::2], -SWIGLU_LIMIT, SWIGLU_LIMIT)
        glu = gate * jax.nn.sigmoid(gate * SWIGLU_ALPHA)
        return ((up + 1) * glu) @ w_dn[e] + b_dn[e]

    ys = lax.map(expert_block, (xs, blk_e)).reshape(n_pad, d)
    out = jnp.zeros((n_tok + 1, d), ys.dtype).at[tok_buf].add(ys * w_buf[:, None].astype(ys.dtype))
    return out[:n_tok]


def setup_inputs(seed: int = 0) -> dict:
    key = jax.random.key(seed)
    ks = jax.random.split(key, 24)
    n_a = (DEPTH + 1) // 2
    n_b = DEPTH // 2
    D = D_MODEL

    def nrm(k, shape, s):
        return jax.random.normal(k, shape, F32) * s

    dt = jnp.exp(jax.random.uniform(ks[10], (n_a, 2, GDN_HEADS), F32, math.log(1e-3), math.log(1e-1)))
    return {
        'x': nrm(ks[0], (BATCH, SEQ, D), 1.0),
        'c': nrm(ks[1], (BATCH, D), 1.0),
        'ctx': nrm(ks[2], (BATCH, CTX_LEN, D), 1.0),
        'c_ctx': nrm(ks[3], (D,), 1.0),
        'w_ada': nrm(ks[4], (DEPTH, D, 6 * D), 0.5 * D ** -0.5),
        'b_ada': nrm(ks[5], (DEPTH, 6 * D), 0.01),
        'g_mix': 1.0 + nrm(ks[6], (DEPTH, D), 0.02),
        'g_ffn': 1.0 + nrm(ks[7], (DEPTH, D), 0.02),
        'gdn_w_in': nrm(ks[8], (n_a, D, GDN_IN), D ** -0.5),
        'gdn_conv': nrm(ks[9], (n_a, GDN_CONV, GDN_QKV), GDN_CONV ** -0.5),
        'gdn_a_log': jnp.log(jax.random.uniform(ks[11], (n_a, 2, GDN_HEADS), F32, 1.0, 16.0)),
        'gdn_dt_bias': dt + jnp.log(-jnp.expm1(-dt)),
        'gdn_onorm': 1.0 + nrm(ks[12], (n_a, GDN_DV), 0.02),
        'gdn_w_out': nrm(ks[13], (n_a, GDN_Z, D), GDN_Z ** -0.5),
        'fnet_w_out': nrm(ks[14], (n_b, D, D), D ** -0.5),
        'moe_w_router': nrm(ks[15], (DEPTH, D, N_EXPERTS), D ** -0.5),
        'moe_b_router': nrm(ks[16], (DEPTH, N_EXPERTS), 0.01),
        'moe_w_gu': nrm(ks[17], (DEPTH, N_EXPERTS, D, 2 * D_FF), D ** -0.5),
        'moe_b_gu': nrm(ks[18], (DEPTH, N_EXPERTS, 2 * D_FF), 0.01),
        'moe_w_down': nrm(ks[19], (DEPTH, N_EXPERTS, D_FF, D), D_FF ** -0.5),
        'moe_b_down': nrm(ks[20], (DEPTH, N_EXPERTS, D), 0.01),
        'g_final': 1.0 + nrm(ks[21], (D,), 0.02),
    }


def reference(x, c, ctx, c_ctx, w_ada, b_ada, g_mix, g_ffn, gdn_w_in, gdn_conv, gdn_a_log, gdn_dt_bias,
              gdn_onorm, gdn_w_out, fnet_w_out, moe_w_router, moe_b_router, moe_w_gu, moe_b_gu,
              moe_w_down, moe_b_down, g_final):
    b, n_lat, d = x.shape
    n_ctx = ctx.shape[1]
    x = x + _grid_pos_embed(n_lat, d).astype(x.dtype)[None]
    for i in range(DEPTH):
        keep_ctx = i < DEPTH - 1
        sh, sc, gt, sh2, sc2, gt2 = [t[:, None, :] for t in _adaln(c, w_ada[i], b_ada[i])]
        csh, csc, cgt, csh2, csc2, cgt2 = _adaln(c_ctx, w_ada[i], b_ada[i])
        j = i // N_MIXERS
        h_lat = _modulate(x, g_mix[i], sh, sc)
        h_ctx = _modulate(ctx, g_mix[i], csh, csc)
        if i % N_MIXERS == 0:
            qc, kc, vc, zc, bc, gcx = _gdn_project(h_ctx, gdn_w_in[j], gdn_conv[j], gdn_a_log[j], gdn_dt_bias[j])
            s0 = jnp.zeros((b, GDN_HEADS, GDN_DK, GDN_DV), F32)
            o_c, s_f, s_b = _gdn_bidir(qc, kc, vc, bc, gcx, s0, s0)
            ql, kl, vl, zl, bl, glx = _gdn_project(h_lat, gdn_w_in[j], gdn_conv[j], gdn_a_log[j], gdn_dt_bias[j])
            o_l, _, _ = _gdn_bidir(ql, kl, vl, bl, glx, s_f, s_b)
            y_lat = _gdn_output(o_l, zl, gdn_onorm[j], gdn_w_out[j])
            if keep_ctx:
                y_ctx = _gdn_output(o_c, zc, gdn_onorm[j], gdn_w_out[j])
        else:
            y_lat = _fourier_mix(h_lat, fnet_w_out[j])
            if keep_ctx:
                y_ctx = _fourier_mix(h_ctx, fnet_w_out[j])
        x = x + gt * y_lat
        h_lat = _modulate(x, g_ffn[i], sh2, sc2)
        if keep_ctx:
            ctx = ctx + cgt * y_ctx
            h_ctx = _modulate(ctx, g_ffn[i], csh2, csc2)
            tokens = jnp.concatenate([h_ctx.reshape(-1, d), h_lat.reshape(-1, d)], axis=0)
            y = _moe(tokens, moe_w_router[i], moe_b_router[i], moe_w_gu[i], moe_b_gu[i], moe_w_down[i], moe_b_down[i])
            ctx = ctx + cgt2 * y[:b * n_ctx].reshape(b, n_ctx, d)
            x = x + gt2 * y[b * n_ctx:].reshape(b, n_lat, d)
        else:
            y = _moe(h_lat.reshape(-1, d), moe_w_router[i], moe_b_router[i], moe_w_gu[i], moe_b_gu[i], moe_w_down[i], moe_b_down[i])
            x = x + gt2 * y.reshape(b, n_lat, d)
    return _rmsnorm(x, g_final)
```

```python
import functools
import math

import jax
import jax.numpy as jnp
from jax import lax
from jax.experimental import pallas as pl
from jax.experimental.pallas import tpu as pltpu

F32 = jnp.float32
BF16 = jnp.bfloat16

GRID_W = 64
N_MIXERS = 2
NORM_EPS = 1e-6
POS_BASE = 10000.0
GDN_HEADS = 8
GDN_DK = 128
GDN_DV = 128
GDN_CONV = 5
GDN_CHUNK = 64
GDN_QKV = GDN_HEADS * (2 * GDN_DK + GDN_DV)
GDN_Z = GDN_HEADS * GDN_DV
FNET_GROUPS = 4
N_EXPERTS = 32
TOP_K = 4
SWIGLU_LIMIT = 7.0
SWIGLU_ALPHA = 1.702

MOE_ROWS = 256
VMEM_LIMIT_BYTES = 56 * 1024 * 1024


def _rmsnorm(x, gain):
    xf = x.astype(F32)
    y = xf * lax.rsqrt(jnp.mean(xf * xf, axis=-1, keepdims=True) + NORM_EPS)
    return (y * gain.astype(F32)).astype(x.dtype)


def _modulate(x, gain, shift, scale):
    return _rmsnorm(x, gain) * (1 + scale) + shift


def _adaln(cvec, w, b):
    return jnp.split(jax.nn.silu(cvec) @ w + b, 6, axis=-1)


def _sincos(pos, dim):
    half = dim // 2
    omega = 1.0 / (POS_BASE ** (jnp.arange(half, dtype=F32) / half))
    ang = pos.astype(F32)[:, None] * omega[None, :]
    return jnp.concatenate([jnp.sin(ang), jnp.cos(ang)], axis=-1)


def _grid_pos_embed(n_tok, dim):
    rows = n_tok // GRID_W
    r = jnp.repeat(jnp.arange(rows), GRID_W)
    col = jnp.tile(jnp.arange(GRID_W), rows)
    return jnp.concatenate([_sincos(r, dim // 2), _sincos(col, dim // 2)], axis=-1)


def _short_conv(u, w):
    n = u.shape[1]
    p = GDN_CONV // 2
    up = jnp.pad(u, ((0, 0), (p, p), (0, 0)))
    return sum(up[:, j:j + n] * w[j] for j in range(GDN_CONV))


def _l2norm(t):
    return t * lax.rsqrt(jnp.sum(t * t, axis=-1, keepdims=True) + NORM_EPS)


def _gdn_chunk_scan(q, k, v, g, beta, s0):
    b, h, n_tok, dk = q.shape
    dv = v.shape[-1]
    C = GDN_CHUNK
    n = n_tok // C
    q = q.reshape(b, h, n, C, dk)
    k = k.reshape(b, h, n, C, dk)
    v = v.reshape(b, h, n, C, dv)
    g = g.reshape(b, h, n, C)
    beta = beta.reshape(b, h, n, C)
    gc = jnp.cumsum(g, axis=-1)
    lower = jnp.tril(jnp.ones((C, C), bool))
    strict = jnp.tril(jnp.ones((C, C), bool), -1)
    decay = jnp.where(lower, jnp.exp(jnp.where(lower, gc[..., :, None] - gc[..., None, :], 0.0)), 0.0)
    kb = k * beta[..., None]
    a_mat = jnp.where(strict, jnp.einsum('bhncd,bhnsd->bhncs', kb, k) * decay, 0.0) + jnp.eye(C, dtype=F32)
    rhs = jnp.concatenate([v * beta[..., None], kb * jnp.exp(gc)[..., None]], axis=-1)
    sol = lax.linalg.triangular_solve(a_mat, rhs, left_side=True, lower=True, unit_diagonal=True)
    u, w = sol[..., :dv], sol[..., dv:]
    attn = jnp.where(lower, jnp.einsum('bhncd,bhnsd->bhncs', q, k) * decay, 0.0)

    def step(S, inp):
        q_i, k_i, u_i, w_i, gc_i, attn_i = inp
        v_new = u_i - jnp.einsum('bhck,bhkv->bhcv', w_i, S)
        o_i = (jnp.einsum('bhck,bhkv->bhcv', q_i * jnp.exp(gc_i)[..., None], S)
               + jnp.einsum('bhcs,bhsv->bhcv', attn_i, v_new))
        g_last = gc_i[..., -1]
        S = (S * jnp.exp(g_last)[..., None, None]
             + jnp.einsum('bhck,bhcv->bhkv', k_i * jnp.exp(g_last[..., None] - gc_i)[..., None], v_new))
        return S, o_i

    xs = tuple(jnp.moveaxis(t, 2, 0) for t in (q, k, u, w, gc, attn))
    S, o = lax.scan(step, s0, xs)
    return jnp.moveaxis(o, 0, 2).reshape(b, h, n_tok, dv), S


def _gdn_project(h, w_in, conv_w, a_log, dt_bias):
    b, n, _ = h.shape
    proj = h @ w_in
    qkv = jax.nn.silu(_short_conv(proj[..., :GDN_QKV], conv_w)).astype(F32)
    z = proj[..., GDN_QKV:GDN_QKV + GDN_Z]
    gates = proj[..., GDN_QKV + GDN_Z:].astype(F32).reshape(b, n, 4, GDN_HEADS)
    gates = jnp.transpose(gates, (2, 0, 3, 1))
    hk = GDN_HEADS * GDN_DK

    def heads(t, d):
        return jnp.transpose(t.reshape(b, n, GDN_HEADS, d), (0, 2, 1, 3))

    q = _l2norm(heads(qkv[..., :hk], GDN_DK)) * (GDN_DK ** -0.5)
    k = _l2norm(heads(qkv[..., hk:2 * hk], GDN_DK))
    v = heads(qkv[..., 2 * hk:], GDN_DV)
    beta = jax.nn.sigmoid(gates[:2])
    g = -jnp.exp(a_log.astype(F32))[:, None, :, None] * jax.nn.softplus(
        gates[2:] + dt_bias.astype(F32)[:, None, :, None])
    return q, k, v, z, beta, g


def _gdn_bidir(q, k, v, beta, g, s_fwd, s_bwd):
    o_f, s_f = _gdn_chunk_scan(q, k, v, g[0], beta[0], s_fwd)

    def fl(t):
        return jnp.flip(t, axis=2)

    o_b, s_b = _gdn_chunk_scan(fl(q), fl(k), fl(v), fl(g[1]), fl(beta[1]), s_bwd)
    return o_f + fl(o_b), s_f, s_b


def _gdn_output(o, z, onorm, w_out):
    b, hh, n, dv = o.shape
    o = jnp.transpose(o, (0, 2, 1, 3))
    o = o * lax.rsqrt(jnp.mean(o * o, axis=-1, keepdims=True) + NORM_EPS) * onorm.astype(F32)
    o = o.reshape(b, n, hh * dv) * jax.nn.silu(z.astype(F32))
    return o.astype(z.dtype) @ w_out


def _fourier_mix(h, w_out):
    b, n, d = h.shape
    hg = jnp.transpose(h.astype(F32).reshape(b, n, FNET_GROUPS, d // FNET_GROUPS), (0, 2, 1, 3))
    f = jnp.fft.fft2(hg, norm='ortho').real
    return jnp.transpose(f, (0, 2, 1, 3)).reshape(b, n, d).astype(h.dtype) @ w_out


def _expert_block_kernel(blk_e_ref, x_ref, wg_ref, wu_ref, bg_ref, bu_ref, wd_ref, bd_ref, o_ref):
    del blk_e_ref
    x = x_ref[...]
    gate = jnp.dot(x, wg_ref[0], preferred_element_type=F32) + bg_ref[0]
    up = jnp.dot(x, wu_ref[0], preferred_element_type=F32) + bu_ref[0]
    gate = jnp.minimum(gate, SWIGLU_LIMIT)
    up = jnp.clip(up, -SWIGLU_LIMIT, SWIGLU_LIMIT)
    glu = gate / (1.0 + jnp.exp(-SWIGLU_ALPHA * gate))
    hid = ((up + 1.0) * glu).astype(BF16)
    o_ref[...] = jnp.dot(hid, wd_ref[0], preferred_element_type=F32) + bd_ref[0]


def _expert_blocks(xs, blk_e, wg, wu, bg, bu, wd, bd):
    n_pad, d = xs.shape
    d_ff = wg.shape[-1]
    n_blk = n_pad // MOE_ROWS
    grid_spec = pltpu.PrefetchScalarGridSpec(
        num_scalar_prefetch=1,
        grid=(n_blk,),
        in_specs=[
            pl.BlockSpec((MOE_ROWS, d), lambda i, be: (i, 0)),
            pl.BlockSpec((1, d, d_ff), lambda i, be: (be[i], 0, 0)),
            pl.BlockSpec((1, d, d_ff), lambda i, be: (be[i], 0, 0)),
            pl.BlockSpec((1, 1, d_ff), lambda i, be: (be[i], 0, 0)),
            pl.BlockSpec((1, 1, d_ff), lambda i, be: (be[i], 0, 0)),
            pl.BlockSpec((1, d_ff, d), lambda i, be: (be[i], 0, 0)),
            pl.BlockSpec((1, 1, d), lambda i, be: (be[i], 0, 0)),
        ],
        out_specs=pl.BlockSpec((MOE_ROWS, d), lambda i, be: (i, 0)),
    )
    return pl.pallas_call(
        _expert_block_kernel,
        grid_spec=grid_spec,
        out_shape=jax.ShapeDtypeStruct((n_pad, d), F32),
        compiler_params=pltpu.CompilerParams(
            dimension_semantics=("arbitrary",), vmem_limit_bytes=VMEM_LIMIT_BYTES),
        name="moe_expert_blocks",
    )(blk_e, xs, wg, wu, bg, bu, wd, bd)


def _moe(h, w_r, b_r, w_gu, b_gu, w_dn, b_dn):
    n_tok, d = h.shape
    n_exp = w_r.shape[-1]
    d_ff = w_dn.shape[1]
    logits = (h @ w_r + b_r).astype(F32)
    top_val, top_idx = lax.top_k(logits, TOP_K)
    top_w = jax.nn.softmax(top_val, axis=-1)
    n_as = n_tok * TOP_K
    flat_e = top_idx.reshape(-1)
    onehot = (flat_e[:, None] == jnp.arange(n_exp, dtype=flat_e.dtype)[None, :]).astype(jnp.int32)
    rank = jnp.take_along_axis(jnp.cumsum(onehot, axis=0), flat_e[:, None], axis=1)[:, 0] - 1
    counts = jnp.sum(onehot, axis=0)
    padded = (counts + MOE_ROWS - 1) // MOE_ROWS * MOE_ROWS
    pad_end = jnp.cumsum(padded)
    pad_start = pad_end - padded
    slot = pad_start[flat_e] + rank
    n_blk = -(-(n_as + n_exp * (MOE_ROWS - 1)) // MOE_ROWS)
    n_pad = n_blk * MOE_ROWS
    tok_buf = jnp.zeros((n_pad,), jnp.int32).at[slot].set(jnp.arange(n_as, dtype=jnp.int32) // TOP_K)
    blk_e = jnp.minimum(jnp.searchsorted(pad_end, jnp.arange(n_blk) * MOE_ROWS, side='right'),
                        n_exp - 1).astype(jnp.int32)
    xs = h.astype(BF16)[tok_buf]
    wg = w_gu[:, :, 0::2].astype(BF16)
    wu = w_gu[:, :, 1::2].astype(BF16)
    bg = b_gu[:, None, 0::2]
    bu = b_gu[:, None, 1::2]
    ys = _expert_blocks(xs, blk_e, wg, wu, bg, bu, w_dn.astype(BF16), b_dn[:, None, :])
    picked = ys[slot].reshape(n_tok, TOP_K, d)
    return jnp.sum(picked * top_w[:, :, None], axis=1)


def kernel(x, c, ctx, c_ctx, w_ada, b_ada, g_mix, g_ffn, gdn_w_in, gdn_conv, gdn_a_log, gdn_dt_bias,
           gdn_onorm, gdn_w_out, fnet_w_out, moe_w_router, moe_b_router, moe_w_gu, moe_b_gu,
           moe_w_down, moe_b_down, g_final):
    b, n_lat, d = x.shape
    n_ctx = ctx.shape[1]
    depth = w_ada.shape[0]
    x = x + _grid_pos_embed(n_lat, d).astype(x.dtype)[None]
    for i in range(depth):
        keep_ctx = i < depth - 1
        sh, sc, gt, sh2, sc2, gt2 = [t[:, None, :] for t in _adaln(c, w_ada[i], b_ada[i])]
        csh, csc, cgt, csh2, csc2, cgt2 = _adaln(c_ctx, w_ada[i], b_ada[i])
        j = i // N_MIXERS
        h_lat = _modulate(x, g_mix[i], sh, sc)
        h_ctx = _modulate(ctx, g_mix[i], csh, csc)
        if i % N_MIXERS == 0:
            qc, kc, vc, zc, bc, gcx = _gdn_project(h_ctx, gdn_w_in[j], gdn_conv[j], gdn_a_log[j], gdn_dt_bias[j])
            s0 = jnp.zeros((b, GDN_HEADS, GDN_DK, GDN_DV), F32)
            o_c, s_f, s_b = _gdn_bidir(qc, kc, vc, bc, gcx, s0, s0)
            ql, kl, vl, zl, bl, glx = _gdn_project(h_lat, gdn_w_in[j], gdn_conv[j], gdn_a_log[j], gdn_dt_bias[j])
            o_l, _, _ = _gdn_bidir(ql, kl, vl, bl, glx, s_f, s_b)
            y_lat = _gdn_output(o_l, zl, gdn_onorm[j], gdn_w_out[j])
            if keep_ctx:
                y_ctx = _gdn_output(o_c, zc, gdn_onorm[j], gdn_w_out[j])
        else:
            y_lat = _fourier_mix(h_lat, fnet_w_out[j])
            if keep_ctx:
                y_ctx = _fourier_mix(h_ctx, fnet_w_out[j])
        x = x + gt * y_lat
        h_lat = _modulate(x, g_ffn[i], sh2, sc2)
        moe_w = (moe_w_router[i], moe_b_router[i], moe_w_gu[i], moe_b_gu[i], moe_w_down[i], moe_b_down[i])
        if keep_ctx:
            ctx = ctx + cgt * y_ctx
            h_ctx = _modulate(ctx, g_ffn[i], csh2, csc2)
            tokens = jnp.concatenate([h_ctx.reshape(-1, d), h_lat.reshape(-1, d)], axis=0)
            y = _moe(tokens, *moe_w)
            ctx = ctx + cgt2 * y[:b * n_ctx].reshape(b, n_ctx, d)
            x = x + gt2 * y[b * n_ctx:].reshape(b, n_lat, d)
        else:
            y = _moe(h_lat.reshape(-1, d), *moe_w)
            x = x + gt2 * y.reshape(b, n_lat, d)
    return _rmsnorm(x, g_final)
```
